```python
import jax
import jax.numpy as jnp
from jax import lax
import numpy as np

D_MODEL = 4096
BATCH = 4
SEQ = 2048
DEPTH = 1

GRID_W = 64
CTX_LEN = 256
EPS = 1e-6
N_HEADS = 16
Q_RANK = 1024
KV_RANK = 512
NOPE_DIM = 128
ROPE_DIM = 64
V_DIM = 128
ROPE_BASE = 10000.0
Q_BLOCK = 128
CONV_DIM = 2048
CONV_WIDTH = 31
N_EXPERTS = 32
TOP_K = 4
EXPERT_FF = 1536
SWIGLU_LIMIT = 7.0
SWIGLU_ALPHA = 1.702

_KV0 = Q_RANK
_PE0 = _KV0 + KV_RANK
_CV0 = _PE0 + ROPE_DIM
_G0 = _CV0 + 2 * CONV_DIM
IN_COLS = _G0 + 2 * D_MODEL

kernel_name = "hybrid_mla_conformer_moe_block"


def rmsnorm(x, g):
    xf = x.astype(jnp.float32)
    y = xf * lax.rsqrt(jnp.mean(xf * xf, axis=-1, keepdims=True) + EPS)
    return (y * g.astype(jnp.float32)).astype(x.dtype)


def layernorm(x, g, b):
    xf = x.astype(jnp.float32)
    mu = jnp.mean(xf, axis=-1, keepdims=True)
    var = jnp.mean(jnp.square(xf - mu), axis=-1, keepdims=True)
    y = (xf - mu) * lax.rsqrt(var + EPS)
    return (y * g.astype(jnp.float32) + b.astype(jnp.float32)).astype(x.dtype)


def modulate(h, shift, scale):
    return h * (1 + scale) + shift


def axial_rope_tables(rows, dtype):
    row = jnp.broadcast_to(jnp.arange(rows, dtype=jnp.float32)[:, None], (rows, GRID_W)).reshape(-1)
    col = jnp.broadcast_to(jnp.arange(GRID_W, dtype=jnp.float32)[None, :], (rows, GRID_W)).reshape(-1)
    n_pairs = ROPE_DIM // 4
    inv_freq = ROPE_BASE ** (-jnp.arange(n_pairs, dtype=jnp.float32) / n_pairs)
    ang = jnp.concatenate([row[:, None] * inv_freq, col[:, None] * inv_freq], axis=-1)
    return jnp.cos(ang).astype(dtype), jnp.sin(ang).astype(dtype)


def apply_rope(x, cos, sin):
    half = ROPE_DIM // 2
    x1, x2 = x[..., :half], x[..., half:]
    return jnp.concatenate([x1 * cos - x2 * sin, x1 * sin + x2 * cos], axis=-1)


def mla_queries(q_lat, p):
    q = rmsnorm(q_lat, p['q_norm_g']) @ p['w_uq']
    q = q.reshape(q.shape[:-1] + (N_HEADS, NOPE_DIM + ROPE_DIM))
    return q[..., :NOPE_DIM], q[..., NOPE_DIM:]


def mla_keys_values(kv_lat, p):
    kv = rmsnorm(kv_lat, p['kv_norm_g']) @ p['w_ukv']
    kv = kv.reshape(kv.shape[:-1] + (N_HEADS, NOPE_DIM + V_DIM))
    return kv[..., :NOPE_DIM], kv[..., NOPE_DIM:]


def context_keys(hc, p):
    zk = hc @ p['w_in'][:, _KV0:_CV0]
    k_nope, v = mla_keys_values(zk[..., :KV_RANK], p)
    return k_nope, zk[..., KV_RANK:], v


def mla_attend(q_nope, q_pe, k_nope, k_pe, v):
    b, sq = q_nope.shape[:2]
    nblk = sq // Q_BLOCK
    scale = (NOPE_DIM + ROPE_DIM) ** -0.5

    def block(qs):
        qn, qp = qs
        s = jnp.einsum('bqhd,bkhd->bhqk', qn, k_nope) + jnp.einsum('bqhr,bkr->bhqk', qp, k_pe)
        prob = jax.nn.softmax(s.astype(jnp.float32) * scale, axis=-1).astype(v.dtype)
        return jnp.einsum('bhqk,bkhd->bqhd', prob, v)

    def to_blocks(t):
        return t.reshape((b, nblk, Q_BLOCK) + t.shape[2:]).swapaxes(0, 1)

    o = lax.map(block, (to_blocks(q_nope), to_blocks(q_pe)))
    return o.swapaxes(0, 1).reshape(b, sq, N_HEADS * V_DIM)


def conformer_conv(u, p):
    z = u[..., :CONV_DIM] * jax.nn.sigmoid(u[..., CONV_DIM:])
    z = lax.conv_general_dilated(
        z, p['w_dw'][:, None, :], window_strides=(1,),
        padding=[(CONV_WIDTH // 2, CONV_WIDTH // 2)],
        dimension_numbers=('NWC', 'WIO', 'NWC'),
        feature_group_count=CONV_DIM) + p['b_dw']
    z = jax.nn.silu(layernorm(z, p['cln_g'], p['cln_b']))
    return z @ p['w_pw'] + p['b_pw']


def token_mixer(h, rope, ctx_kv, p):
    z = h @ p['w_in']
    q_nope, q_pe = mla_queries(z[..., :_KV0], p)
    k_nope, v = mla_keys_values(z[..., _KV0:_PE0], p)
    k_pe = z[..., _PE0:_CV0]
    if rope is not None:
        cos, sin = rope
        q_pe = apply_rope(q_pe, cos[:, None, :], sin[:, None, :])
        k_pe = apply_rope(k_pe, cos, sin)
    if ctx_kv is not None:
        ck_nope, ck_pe, cv = ctx_kv
        k_nope = jnp.concatenate([ck_nope, k_nope], axis=1)
        k_pe = jnp.concatenate([ck_pe, k_pe], axis=1)
        v = jnp.concatenate([cv, v], axis=1)
    y_attn = mla_attend(q_nope, q_pe, k_nope, k_pe, v) @ p['w_o_attn']
    y_conv = conformer_conv(z[..., _CV0:_G0], p)
    gates = jax.nn.sigmoid(z[..., _G0:])
    merged = gates[..., :D_MODEL] * y_attn + gates[..., D_MODEL:] * y_conv
    return merged @ p['w_out']


def moe(h, p):
    b, s, d = h.shape
    t = h.reshape(b * s, d)
    logits = (t @ p['w_router'] + p['b_router']).astype(jnp.float32)
    top_val, top_idx = lax.top_k(logits, TOP_K)
    top_w = jax.nn.softmax(top_val, axis=-1)
    combine = jnp.sum(jax.nn.one_hot(top_idx, N_EXPERTS, dtype=jnp.float32) * top_w[..., None], axis=1)
    combine = combine.astype(h.dtype)

    def expert(acc, xs):
        wgu, bgu, wd, bd, ce = xs
        gu = t @ wgu + bgu
        g = jnp.minimum(gu[:, :EXPERT_FF], SWIGLU_LIMIT)
        u = jnp.clip(gu[:, EXPERT_FF:], -SWIGLU_LIMIT, SWIGLU_LIMIT)
        y = ((u + 1) * g * jax.nn.sigmoid(SWIGLU_ALPHA * g)) @ wd + bd
        return acc + ce[:, None] * y, None

    acc, _ = lax.scan(expert, jnp.zeros_like(t),
                      (p['w_gu'], p['b_gu'], p['w_down'], p['b_down'], combine.T))
    return acc.reshape(b, s, d)


def setup_inputs(seed: int = 0) -> dict:
    key = jax.random.key(seed)
    ks = iter(jax.random.split(key, 40))
    L = DEPTH

    def nrm(shape, scale):
        return jax.random.normal(next(ks), shape, jnp.float32) * scale

    def gain(shape):
        return 1.0 + nrm(shape, 0.05)

    return {
        'x': nrm((BATCH, SEQ, D_MODEL), 1.0),
        'c': nrm((BATCH, D_MODEL), 1.0),
        'ctx': nrm((BATCH, CTX_LEN, D_MODEL), 1.0),
        'c_ctx': nrm((D_MODEL,), 1.0),
        'w_ada': nrm((L, D_MODEL, 6 * D_MODEL), 0.5 * D_MODEL ** -0.5),
        'b_ada': nrm((L, 6 * D_MODEL), 0.02),
        'pre1_g': gain((L, D_MODEL)),
        'post1_g': gain((L, D_MODEL)),
        'pre2_g': gain((L, D_MODEL)),
        'post2_g': gain((L, D_MODEL)),
        'w_in': nrm((L, D_MODEL, IN_COLS), D_MODEL ** -0.5),
        'q_norm_g': gain((L, Q_RANK)),
        'w_uq': nrm((L, Q_RANK, N_HEADS * (NOPE_DIM + ROPE_DIM)), Q_RANK ** -0.5),
        'kv_norm_g': gain((L, KV_RANK)),
        'w_ukv': nrm((L, KV_RANK, N_HEADS * (NOPE_DIM + V_DIM)), KV_RANK ** -0.5),
        'w_o_attn': nrm((L, N_HEADS * V_DIM, D_MODEL), (N_HEADS * V_DIM) ** -0.5),
        'w_dw': nrm((L, CONV_WIDTH, CONV_DIM), CONV_WIDTH ** -0.5),
        'b_dw': nrm((L, CONV_DIM), 0.02),
        'cln_g': gain((L, CONV_DIM)),
        'cln_b': nrm((L, CONV_DIM), 0.02),
        'w_pw': nrm((L, CONV_DIM, D_MODEL), CONV_DIM ** -0.5),
        'b_pw': nrm((L, D_MODEL), 0.02),
        'w_out': nrm((L, D_MODEL, D_MODEL), D_MODEL ** -0.5),
        'w_router': nrm((L, D_MODEL, N_EXPERTS), D_MODEL ** -0.5),
        'b_router': nrm((L, N_EXPERTS), 0.01),
        'w_gu': nrm((L, N_EXPERTS, D_MODEL, 2 * EXPERT_FF), D_MODEL ** -0.5),
        'b_gu': nrm((L, N_EXPERTS, 2 * EXPERT_FF), 0.02),
        'w_down': nrm((L, N_EXPERTS, EXPERT_FF, D_MODEL), EXPERT_FF ** -0.5),
        'b_down': nrm((L, N_EXPERTS, D_MODEL), 0.02),
    }


def reference(x, c, ctx, c_ctx, w_ada, b_ada, pre1_g, post1_g, pre2_g, post2_g,
              w_in, q_norm_g, w_uq, kv_norm_g, w_ukv, w_o_attn,
              w_dw, b_dw, cln_g, cln_b, w_pw, b_pw, w_out,
              w_router, b_router, w_gu, b_gu, w_down, b_down):
    rows = x.shape[1] // GRID_W
    rope = axial_rope_tables(rows, x.dtype)
    for l in range(DEPTH):
        p = {
            'w_in': w_in[l], 'q_norm_g': q_norm_g[l], 'w_uq': w_uq[l],
            'kv_norm_g': kv_norm_g[l], 'w_ukv': w_ukv[l], 'w_o_attn': w_o_attn[l],
            'w_dw': w_dw[l], 'b_dw': b_dw[l], 'cln_g': cln_g[l], 'cln_b': cln_b[l],
            'w_pw': w_pw[l], 'b_pw': b_pw[l], 'w_out': w_out[l],
            'w_router': w_router[l], 'b_router': b_router[l],
            'w_gu': w_gu[l], 'b_gu': b_gu[l], 'w_down': w_down[l], 'b_down': b_down[l],
        }
        last = l == DEPTH - 1
        mod = jax.nn.silu(c) @ w_ada[l] + b_ada[l]
        sh1, sc1, g1, sh2, sc2, g2 = jnp.split(mod[:, None, :], 6, axis=-1)
        csh1, csc1, cg1, csh2, csc2, cg2 = jnp.split(jax.nn.silu(c_ctx) @ w_ada[l] + b_ada[l], 6)
        h = modulate(rmsnorm(x, pre1_g[l]), sh1, sc1)
        hc = modulate(rmsnorm(ctx, pre1_g[l]), csh1, csc1)
        y = token_mixer(h, rope, context_keys(hc, p), p)
        x = x + g1 * rmsnorm(y, post1_g[l])
        h2 = modulate(rmsnorm(x, pre2_g[l]), sh2, sc2)
        x = x + g2 * rmsnorm(moe(h2, p), post2_g[l])
        if not last:
            ctx = ctx + cg1 * rmsnorm(token_mixer(hc, None, None, p), post1_g[l])
            hc2 = modulate(rmsnorm(ctx, pre2_g[l]), csh2, csc2)
            ctx = ctx + cg2 * rmsnorm(moe(hc2, p), post2_g[l])
    return x
```

```python
import functools

import jax
import jax.numpy as jnp
from jax import lax
from jax.experimental import pallas as pl
from jax.experimental.pallas import tpu as pltpu

F32 = jnp.float32
BF16 = jnp.bfloat16

D_MODEL = 4096
BATCH = 4
SEQ = 2048
TOKENS = BATCH * SEQ
GRID_W = 64
CTX_LEN = 256
EPS = 1e-6
N_HEADS = 16
Q_RANK = 1024
KV_RANK = 512
NOPE_DIM = 128
ROPE_DIM = 64
V_DIM = 128
ROPE_BASE = 10000.0
CONV_DIM = 2048
CONV_WIDTH = 31
N_EXPERTS = 32
TOP_K = 4
EXPERT_FF = 1536
SWIGLU_LIMIT = 7.0
SWIGLU_ALPHA = 1.702

KV0 = Q_RANK
PE0 = KV0 + KV_RANK
CV0 = PE0 + ROPE_DIM
G0 = CV0 + 2 * CONV_DIM

LANE = 128
HEAD_PAD = 2 * LANE
KV_COLS = KV_RANK + LANE
ATT_SCALE = (NOPE_DIM + ROPE_DIM) ** -0.5
HALF = D_MODEL // 2

MOE_TILE = 256
MOE_FF_CHUNK = 512
MOE_DOWN_COLS = 2048
MOE_MAX_TILES = TOKENS * TOP_K // MOE_TILE + N_EXPERTS
MOE_ROWS = MOE_MAX_TILES * MOE_TILE
ROUTER_ROWS = LANE

VMEM_LIMIT_V7X = 56 * 1024 * 1024


def _cp(semantics, vmem=VMEM_LIMIT_V7X):
    return pltpu.CompilerParams(dimension_semantics=semantics, vmem_limit_bytes=vmem)


def _rms(x):
    return x * lax.rsqrt(jnp.mean(x * x, axis=-1, keepdims=True) + EPS)


def _dot(a, b):
    return jnp.dot(a, b, preferred_element_type=F32)


def _dot_nt(a, b):
    return lax.dot_general(a, b, (((1,), (1,)), ((), ())), preferred_element_type=F32)


def _ada_kernel(c_ref, w_ref, b_ref, o_ref):
    c = c_ref[...]
    a = (c * jax.nn.sigmoid(c)).astype(BF16)
    o_ref[...] = _dot(a, w_ref[...].astype(BF16)) + b_ref[...]


def _ada(cc, w, b):
    tn = 512
    n = w.shape[1]
    return pl.pallas_call(
        _ada_kernel,
        out_shape=jax.ShapeDtypeStruct((8, n), F32),
        grid=(n // tn,),
        in_specs=[
            pl.BlockSpec((8, D_MODEL), lambda j: (0, 0)),
            pl.BlockSpec((D_MODEL, tn), lambda j: (0, j)),
            pl.BlockSpec((1, tn), lambda j: (0, j)),
        ],
        out_specs=pl.BlockSpec((8, tn), lambda j: (0, j)),
        compiler_params=_cp(("arbitrary",)),
        name="ada",
    )(cc, w, b)


def _prenorm_kernel(x_ref, g_ref, sc_ref, sh_ref, o_ref):
    y = _rms(x_ref[...]) * g_ref[...]
    o_ref[...] = (y * (1.0 + sc_ref[...]) + sh_ref[...]).astype(BF16)


def _prenorm(x2d, g, mod3, row_of_tile, sh_slot, sc_slot, tm=256):
    m = x2d.shape[0]
    return pl.pallas_call(
        _prenorm_kernel,
        out_shape=jax.ShapeDtypeStruct((m, D_MODEL), BF16),
        grid=(m // tm,),
        in_specs=[
            pl.BlockSpec((tm, D_MODEL), lambda i: (i, 0)),
            pl.BlockSpec((1, D_MODEL), lambda i: (0, 0)),
            pl.BlockSpec((None, 1, D_MODEL), lambda i: (row_of_tile(i) * 6 + sc_slot, 0, 0)),
            pl.BlockSpec((None, 1, D_MODEL), lambda i: (row_of_tile(i) * 6 + sh_slot, 0, 0)),
        ],
        out_specs=pl.BlockSpec((tm, D_MODEL), lambda i: (i, 0)),
        compiler_params=_cp(("arbitrary",)),
        name="prenorm",
    )(x2d, g, mod3, mod3)


def _mm_kernel(a_ref, w_ref, o_ref):
    o_ref[...] = _dot(a_ref[...], w_ref[...]).astype(o_ref.dtype)


def _mm(a, w, tm, tn, out_dtype=BF16, name="mm"):
    m, k = a.shape
    n = w.shape[1]
    return pl.pallas_call(
        _mm_kernel,
        out_shape=jax.ShapeDtypeStruct((m, n), out_dtype),
        grid=(n // tn, m // tm),
        in_specs=[
            pl.BlockSpec((tm, k), lambda j, i: (i, 0)),
            pl.BlockSpec((k, tn), lambda j, i: (0, j)),
        ],
        out_specs=pl.BlockSpec((tm, tn), lambda j, i: (i, j)),
        compiler_params=_cp(("arbitrary", "arbitrary")),
        name=name,
    )(a, w)


def _rope_lanes(x, ct, sa, sb):
    return x * ct + pltpu.roll(x, 96, 1) * sa + pltpu.roll(x, 32, 1) * sb


def _mla_kernel(with_q, *refs):
    if with_q:
        (a_ref, wq_ref, qg_ref, wuq_ref, wkv_ref, kvg_ref, wukv_ref, ct_ref, sa_ref, sb_ref,
         q_ref, k_ref, v_ref) = refs
    else:
        (a_ref, wkv_ref, kvg_ref, wukv_ref, ct_ref, sa_ref, sb_ref, k_ref, v_ref) = refs
    a = a_ref[...]
    ct, sa, sb = ct_ref[...], sa_ref[...], sb_ref[...]
    if with_q:
        qn = (_rms(_dot(a, wq_ref[...])) * qg_ref[...]).astype(BF16)
        for h in range(N_HEADS):
            c0 = h * HEAD_PAD
            r = _dot(qn, wuq_ref[:, c0:c0 + HEAD_PAD])
            q_ref[:, c0:c0 + LANE] = (r[:, :LANE] * ATT_SCALE).astype(BF16)
            rot = _rope_lanes(r[:, LANE:], ct, sa, sb)
            q_ref[:, c0 + LANE:c0 + HEAD_PAD] = (rot * ATT_SCALE).astype(BF16)
    zk = _dot(a, wkv_ref[...])
    kn = (_rms(zk[:, :KV_RANK]) * kvg_ref[...]).astype(BF16)
    rotk = _rope_lanes(zk[:, KV_RANK:], ct, sa, sb).astype(BF16)
    kk = _dot(kn, wukv_ref[:, :N_HEADS * NOPE_DIM])
    for h in range(N_HEADS):
        c0 = h * HEAD_PAD
        k_ref[:, c0:c0 + LANE] = kk[:, h * NOPE_DIM:(h + 1) * NOPE_DIM].astype(BF16)
        k_ref[:, c0 + LANE:c0 + HEAD_PAD] = rotk
    v_ref[...] = _dot(kn, wukv_ref[:, N_HEADS * NOPE_DIM:]).astype(BF16)


def _const_spec(shape):
    return pl.BlockSpec(shape, lambda i: (0,) * len(shape), pipeline_mode=pl.Buffered(1))


def _mla(a, wq, qg, wuq, wkv, kvg, wukv, tabs, tab_block, with_q, tm=256):
    m = a.shape[0]
    ct, sa, sb = tabs
    row = pl.BlockSpec((tm, D_MODEL), lambda i: (i, 0))
    tab = pl.BlockSpec((tm, LANE), lambda i: (tab_block(i), 0))
    in_specs = [row]
    args = [a]
    if with_q:
        in_specs += [_const_spec(wq.shape), _const_spec(qg.shape), _const_spec(wuq.shape)]
        args += [wq, qg, wuq]
    in_specs += [_const_spec(wkv.shape), _const_spec(kvg.shape), _const_spec(wukv.shape), tab, tab, tab]
    args += [wkv, kvg, wukv, ct, sa, sb]
    out_shape = [jax.ShapeDtypeStruct((m, N_HEADS * HEAD_PAD), BF16),
                 jax.ShapeDtypeStruct((m, N_HEADS * V_DIM), BF16)]
    out_specs = [pl.BlockSpec((tm, N_HEADS * HEAD_PAD), lambda i: (i, 0)),
                 pl.BlockSpec((tm, N_HEADS * V_DIM), lambda i: (i, 0))]
    if with_q:
        out_shape = [jax.ShapeDtypeStruct((m, N_HEADS * HEAD_PAD), BF16)] + out_shape
        out_specs = [pl.BlockSpec((tm, N_HEADS * HEAD_PAD), lambda i: (i, 0))] + out_specs
    return pl.pallas_call(
        functools.partial(_mla_kernel, with_q),
        out_shape=out_shape,
        grid=(m // tm,),
        in_specs=in_specs,
        out_specs=out_specs,
        compiler_params=_cp(("arbitrary",)),
        name="mla_q" if with_q else "mla_ctx",
    )(*args)


def _attn_kernel(q_ref, kl_ref, vl_ref, kc_ref, vc_ref, o_ref):
    q = q_ref[...]
    s1 = _dot_nt(q, kl_ref[...])
    s2 = _dot_nt(q, kc_ref[...])
    m = jnp.maximum(jnp.max(s1, axis=-1, keepdims=True), jnp.max(s2, axis=-1, keepdims=True))
    p1 = jnp.exp(s1 - m)
    p2 = jnp.exp(s2 - m)
    den = jnp.sum(p1, axis=-1, keepdims=True) + jnp.sum(p2, axis=-1, keepdims=True)
    o = _dot(p1.astype(BF16), vl_ref[...]) + _dot(p2.astype(BF16), vc_ref[...])
    o_ref[...] = (o / den).astype(BF16)


def _attention(q, kl, vl, kc, vc, tq=512):
    nq = SEQ // tq
    return pl.pallas_call(
        _attn_kernel,
        out_shape=jax.ShapeDtypeStruct((TOKENS, N_HEADS * V_DIM), BF16),
        grid=(BATCH, N_HEADS, nq),
        in_specs=[
            pl.BlockSpec((tq, HEAD_PAD), lambda b, h, i: (b * nq + i, h)),
            pl.BlockSpec((SEQ, HEAD_PAD), lambda b, h, i: (b, h)),
            pl.BlockSpec((SEQ, V_DIM), lambda b, h, i: (b, h)),
            pl.BlockSpec((CTX_LEN, HEAD_PAD), lambda b, h, i: (b, h)),
            pl.BlockSpec((CTX_LEN, V_DIM), lambda b, h, i: (b, h)),
        ],
        out_specs=pl.BlockSpec((tq, V_DIM), lambda b, h, i: (b * nq + i, h)),
        compiler_params=_cp(("arbitrary", "arbitrary", "arbitrary")),
        name="attention",
    )(q, kl, vl, kc, vc)


CONV_TILE = 256
CONV_HALO = 16
CONV_ROWS = 64
CONV_CB = CONV_DIM // LANE


def _conv_kernel(main_ref, prev_ref, next_ref, w_ref, b_ref, g_ref, beta_ref, o_ref, buf, acc):
    tt = CONV_TILE
    si = pl.program_id(0) % (SEQ // tt)

    def glu(u):
        u = u.astype(F32)
        return u[:, :CONV_DIM] * jax.nn.sigmoid(u[:, CONV_DIM:])

    prev = jnp.where(si != 0, glu(prev_ref[...]), 0.0)
    nxt = jnp.where(si != SEQ // tt - 1, glu(next_ref[...]), 0.0)
    cur = glu(main_ref[...])
    for c in range(CONV_CB):
        ls = slice(c * LANE, (c + 1) * LANE)
        buf[c, 0:CONV_HALO, :] = prev[:, ls]
        buf[c, CONV_HALO:CONV_HALO + tt, :] = cur[:, ls]
        buf[c, CONV_HALO + tt:, :] = nxt[:, ls]

    first = CONV_HALO - CONV_WIDTH // 2

    def per_block(c, carry):
        wv = w_ref[c]
        bv = b_ref[c]
        for r0 in range(0, tt, CONV_ROWS):
            a = jnp.zeros((CONV_ROWS, LANE), F32)
            for k in range(CONV_WIDTH):
                a = a + buf[c, pl.ds(r0 + first + k, CONV_ROWS), :] * wv[k:k + 1, :]
            acc[c, pl.ds(r0, CONV_ROWS), :] = a + bv
        return carry

    lax.fori_loop(0, CONV_CB, per_block, 0)

    tot = acc[0]
    for c in range(1, CONV_CB):
        tot = tot + acc[c]
    mu = jnp.sum(tot, axis=-1, keepdims=True) * (1.0 / CONV_DIM)
    sq = jnp.zeros((tt, LANE), F32)
    for c in range(CONV_CB):
        d = acc[c] - mu
        sq = sq + d * d
    rstd = lax.rsqrt(jnp.sum(sq, axis=-1, keepdims=True) * (1.0 / CONV_DIM) + EPS)
    for c in range(CONV_CB):
        ls = slice(c * LANE, (c + 1) * LANE)
        y = (acc[c] - mu) * rstd * g_ref[:, ls] + beta_ref[:, ls]
        o_ref[:, ls] = (y * jax.nn.sigmoid(y)).astype(BF16)


def _conv(zb, w3, b3, g, beta):
    tt = CONV_TILE
    per = tt // CONV_HALO
    last = TOKENS // CONV_HALO - 1
    return pl.pallas_call(
        _conv_kernel,
        out_shape=jax.ShapeDtypeStruct((TOKENS, CONV_DIM), BF16),
        grid=(TOKENS // tt,),
        in_specs=[
            pl.BlockSpec((tt, 2 * CONV_DIM), lambda i: (i, 0)),
            pl.BlockSpec((CONV_HALO, 2 * CONV_DIM), lambda i: (jnp.maximum(i * per - 1, 0), 0)),
            pl.BlockSpec((CONV_HALO, 2 * CONV_DIM), lambda i: (jnp.minimum((i + 1) * per, last), 0)),
            pl.BlockSpec(w3.shape, lambda i: (0, 0, 0)),
            pl.BlockSpec(b3.shape, lambda i: (0, 0, 0)),
            pl.BlockSpec((1, CONV_DIM), lambda i: (0, 0)),
            pl.BlockSpec((1, CONV_DIM), lambda i: (0, 0)),
        ],
        out_specs=pl.BlockSpec((tt, CONV_DIM), lambda i: (i, 0)),
        scratch_shapes=[pltpu.VMEM((CONV_CB, tt + 2 * CONV_HALO, LANE), F32),
                        pltpu.VMEM((CONV_CB, tt, LANE), F32)],
        compiler_params=_cp(("arbitrary",)),
        name="conv",
    )(zb, zb, zb, w3, b3, g, beta)


def _merge_kernel(ao_ref, wo_ref, cv_ref, wpw_ref, bpw_ref, ga_ref, gc_ref, o_ref):
    ya = _dot(ao_ref[...], wo_ref[...])
    yc = _dot(cv_ref[...], wpw_ref[...]) + bpw_ref[...]
    ga = jax.nn.sigmoid(ga_ref[...].astype(F32))
    gc = jax.nn.sigmoid(gc_ref[...].astype(F32))
    o_ref[...] = (ga * ya + gc * yc).astype(BF16)


def _merge(ao, wo, cv, wpw, bpw, zb, tm=512, tn=1024):
    ga0 = 2 * CONV_DIM // tn
    gc0 = (2 * CONV_DIM + D_MODEL) // tn
    return pl.pallas_call(
        _merge_kernel,
        out_shape=jax.ShapeDtypeStruct((TOKENS, D_MODEL), BF16),
        grid=(D_MODEL // tn, TOKENS // tm),
        in_specs=[
            pl.BlockSpec((tm, N_HEADS * V_DIM), lambda j, i: (i, 0)),
            pl.BlockSpec((N_HEADS * V_DIM, tn), lambda j, i: (0, j)),
            pl.BlockSpec((tm, CONV_DIM), lambda j, i: (i, 0)),
            pl.BlockSpec((CONV_DIM, tn), lambda j, i: (0, j)),
            pl.BlockSpec((1, tn), lambda j, i: (0, j)),
            pl.BlockSpec((tm, tn), lambda j, i: (i, ga0 + j)),
            pl.BlockSpec((tm, tn), lambda j, i: (i, gc0 + j)),
        ],
        out_specs=pl.BlockSpec((tm, tn), lambda j, i: (i, j)),
        compiler_params=_cp(("arbitrary", "arbitrary")),
        name="merge",
    )(ao, wo, cv, wpw, bpw, zb, zb)


ROUTER_TILE = 256


def _router_kernel(y_ref, x_ref, p1g_ref, p2g_ref, g1_ref, sc2_ref, sh2_ref, wrh_ref, wrl_ref, br_ref,
                   x1_ref, hp_ref, idx_ref, wt_ref, rank_ref, cnt_ref, carry):
    tm = ROUTER_TILE

    @pl.when(pl.program_id(0) == 0)
    def _():
        carry[...] = jnp.zeros_like(carry)

    x1 = x_ref[...] + g1_ref[...] * (_rms(y_ref[...].astype(F32)) * p1g_ref[...])
    x1_ref[...] = x1
    h2 = _rms(x1) * p2g_ref[...] * (1.0 + sc2_ref[...]) + sh2_ref[...]
    hb = h2.astype(BF16)
    hbf = hb.astype(F32)
    bits = lax.bitcast_convert_type(hbf, jnp.uint32)
    hp_ref[...] = (bits[:, :HALF] >> 16) | bits[:, HALF:]

    hl = (h2 - hbf).astype(BF16)
    wrh = wrh_ref[...]
    logits = _dot_nt(wrh, hb) + _dot_nt(wrh, hl) + _dot_nt(wrl_ref[...], hb)
    l = logits[:N_EXPERTS, :] + br_ref[...]
    eidx = lax.broadcasted_iota(jnp.int32, (N_EXPERTS, tm), 0).astype(F32)
    vals, idxs = [], []
    mask = jnp.zeros((N_EXPERTS, tm), F32)
    for _ in range(TOP_K):
        mx = jnp.max(l, axis=0, keepdims=True)
        ix = jnp.min(jnp.where(l == mx, eidx, float(N_EXPERTS)), axis=0, keepdims=True)
        sel = eidx == ix
        vals.append(mx)
        idxs.append(ix)
        mask = mask + sel.astype(F32)
        l = jnp.where(sel, -jnp.inf, l)
    ex = [jnp.exp(v - vals[0]) for v in vals]
    den = ex[0] + ex[1] + ex[2] + ex[3]

    r_i = lax.broadcasted_iota(jnp.int32, (tm, tm), 0)
    c_i = lax.broadcasted_iota(jnp.int32, (tm, tm), 1)
    upper = (r_i < c_i).astype(BF16)
    before = _dot(mask.astype(BF16), upper) + carry[:, :1]
    for k in range(TOP_K):
        idx_ref[k:k + 1, :] = idxs[k].astype(jnp.int32)
        wt_ref[k:k + 1, :] = ex[k] / den
        rk = jnp.sum(jnp.where(eidx == idxs[k], before, 0.0), axis=0, keepdims=True)
        rank_ref[k:k + 1, :] = rk.astype(jnp.int32)
    carry[...] = carry[...] + jnp.sum(mask, axis=1, keepdims=True)
    cnt_ref[...] = carry[...].astype(jnp.int32)


def _router(y, x2d, p1g, p2g, mod3, wrh, wrl, br):
    tm = ROUTER_TILE
    per = SEQ // tm
    row = lambda i: (i, 0)
    vec = pl.BlockSpec((1, D_MODEL), lambda i: (0, 0))

    def mod(slot):
        return pl.BlockSpec((None, 1, D_MODEL), lambda i: ((i // per) * 6 + slot, 0, 0))

    return pl.pallas_call(
        _router_kernel,
        out_shape=[
            jax.ShapeDtypeStruct((TOKENS, D_MODEL), F32),
            jax.ShapeDtypeStruct((TOKENS, HALF), jnp.uint32),
            jax.ShapeDtypeStruct((TOP_K, TOKENS), jnp.int32),
            jax.ShapeDtypeStruct((TOP_K, TOKENS), F32),
            jax.ShapeDtypeStruct((TOP_K, TOKENS), jnp.int32),
            jax.ShapeDtypeStruct((N_EXPERTS, LANE), jnp.int32),
        ],
        grid=(TOKENS // tm,),
        in_specs=[
            pl.BlockSpec((tm, D_MODEL), row),
            pl.BlockSpec((tm, D_MODEL), row),
            vec, vec, mod(2), mod(4), mod(3),
            pl.BlockSpec((ROUTER_ROWS, D_MODEL), lambda i: (0, 0)),
            pl.BlockSpec((ROUTER_ROWS, D_MODEL), lambda i: (0, 0)),
            pl.BlockSpec((N_EXPERTS, 1), lambda i: (0, 0)),
        ],
        out_specs=[
            pl.BlockSpec((tm, D_MODEL), row),
            pl.BlockSpec((tm, HALF), row),
            pl.BlockSpec((TOP_K, tm), lambda i: (0, i)),
            pl.BlockSpec((TOP_K, tm), lambda i: (0, i)),
            pl.BlockSpec((TOP_K, tm), lambda i: (0, i)),
            pl.BlockSpec((N_EXPERTS, LANE), lambda i: (0, 0)),
        ],
        scratch_shapes=[pltpu.VMEM((N_EXPERTS, LANE), F32)],
        compiler_params=_cp(("arbitrary",)),
        name="router",
    )(y, x2d, p1g, p2g, mod3, mod3, mod3, wrh, wrl, br)


DISPATCH_TILE = 256


def _dispatch_kernel(pos_ref, meta_ref, h_ref, xs_ref, zeros, sem, zsem):
    tm = DISPATCH_TILE
    i = pl.program_id(0)
    base = i * tm

    def row_copy(r, p):
        return pltpu.make_async_copy(h_ref.at[pl.ds(r, 1)], xs_ref.at[pl.ds(p, 1)], sem)

    def zero_row_copy(p):
        return pltpu.make_async_copy(zeros.at[pl.ds(0, 1)], xs_ref.at[pl.ds(p, 1)], zsem)

    def zero_tile_copy(t):
        return pltpu.make_async_copy(zeros, xs_ref.at[pl.ds(t * MOE_TILE, MOE_TILE)], zsem)

    def loop(lo, hi, fn):
        def body(r, carry):
            fn(r)
            return carry

        lax.fori_loop(lo, hi, body, 0)

    @pl.when(i == 0)
    def _():
        zeros[...] = jnp.zeros_like(zeros)

    n_used = meta_ref[2 * N_EXPERTS]
    e = jnp.minimum(i, N_EXPERTS - 1)
    n_pad = jnp.where(i < N_EXPERTS, meta_ref[N_EXPERTS + e], 0)

    @pl.when(i == 0)
    def _():
        loop(n_used, MOE_MAX_TILES, lambda t: zero_tile_copy(t).start())

    loop(0, n_pad, lambda r: zero_row_copy(meta_ref[e] + r).start())

    def issue(r):
        for k in range(TOP_K):
            row_copy(r, pos_ref[k * TOKENS + base + r]).start()

    loop(0, tm, issue)

    def drain(r):
        for k in range(TOP_K):
            row_copy(r, 0).wait()

    loop(0, tm, drain)
    loop(0, n_pad, lambda r: zero_row_copy(0).wait())

    @pl.when(i == 0)
    def _():
        loop(n_used, MOE_MAX_TILES, lambda t: zero_tile_copy(0).wait())


def _dispatch(pos_flat, meta, hp):
    tm = DISPATCH_TILE
    assert TOKENS // tm >= N_EXPERTS and tm == MOE_TILE
    return pl.pallas_call(
        _dispatch_kernel,
        out_shape=jax.ShapeDtypeStruct((MOE_ROWS, HALF), jnp.uint32),
        grid_spec=pltpu.PrefetchScalarGridSpec(
            num_scalar_prefetch=2,
            grid=(TOKENS // tm,),
            in_specs=[pl.BlockSpec((tm, HALF), lambda i, pos, meta: (i, 0))],
            out_specs=pl.BlockSpec(memory_space=pl.ANY),
            scratch_shapes=[pltpu.VMEM((MOE_TILE, HALF), jnp.uint32),
                            pltpu.SemaphoreType.DMA, pltpu.SemaphoreType.DMA],
        ),
        compiler_params=_cp(("arbitrary",)),
        name="dispatch",
    )(pos_flat, meta, hp)


def _cast_rows(src_ref, dst_ref, rows=256):
    n = src_ref.shape[0] // rows

    def body(i, carry):
        r0 = pl.multiple_of(i * rows, rows)
        dst_ref[pl.ds(r0, rows), :] = src_ref[pl.ds(r0, rows), :].astype(dst_ref.dtype)
        return carry

    lax.fori_loop(0, n, body, 0)


def _gateup_kernel(te_ref, tv_ref, nu_ref, xs_ref, wg_ref, wu_ref, bg_ref, bu_ref, o_ref, wg_bf, wu_bf):
    i = pl.program_id(1)
    valid = tv_ref[i]
    new_w = jnp.logical_or(i == 0, te_ref[i] != te_ref[jnp.maximum(i - 1, 0)])

    @pl.when(jnp.logical_and(new_w, valid > 0))
    def _():
        _cast_rows(wg_ref, wg_bf)
        _cast_rows(wu_ref, wu_bf)

    @pl.when(valid == 0)
    def _():
        o_ref[...] = jnp.zeros_like(o_ref)

    @pl.when(valid > 0)
    def _():
        w = xs_ref[...]
        lo = lax.bitcast_convert_type(w << 16, F32).astype(BF16)
        hi = lax.bitcast_convert_type(w & jnp.uint32(0xFFFF0000), F32).astype(BF16)
        g = _dot(lo, wg_bf[:HALF, :]) + _dot(hi, wg_bf[HALF:, :]) + bg_ref[...]
        u = _dot(lo, wu_bf[:HALF, :]) + _dot(hi, wu_bf[HALF:, :]) + bu_ref[...]
        g = jnp.minimum(g, SWIGLU_LIMIT)
        u = jnp.clip(u, -SWIGLU_LIMIT, SWIGLU_LIMIT)
        o_ref[...] = ((u + 1.0) * g * jax.nn.sigmoid(SWIGLU_ALPHA * g)).astype(BF16)


def _gateup(tile_e, tile_valid, n_used, xs, w_gu, b_gu):
    f = MOE_FF_CHUNK
    nj = EXPERT_FF // f

    def tile(i, nu):
        return jnp.minimum(i, nu[0] - 1)

    return pl.pallas_call(
        _gateup_kernel,
        out_shape=jax.ShapeDtypeStruct((MOE_ROWS, EXPERT_FF), BF16),
        grid_spec=pltpu.PrefetchScalarGridSpec(
            num_scalar_prefetch=3,
            grid=(nj, MOE_MAX_TILES),
            in_specs=[
                pl.BlockSpec((MOE_TILE, HALF), lambda j, i, te, tv, nu: (tile(i, nu), 0)),
                pl.BlockSpec((None, D_MODEL, f), lambda j, i, te, tv, nu: (te[i], 0, j)),
                pl.BlockSpec((None, D_MODEL, f), lambda j, i, te, tv, nu: (te[i], 0, nj + j)),
                pl.BlockSpec((None, 1, f), lambda j, i, te, tv, nu: (te[i], 0, j)),
                pl.BlockSpec((None, 1, f), lambda j, i, te, tv, nu: (te[i], 0, nj + j)),
            ],
            out_specs=pl.BlockSpec((MOE_TILE, f), lambda j, i, te, tv, nu: (i, j)),
            scratch_shapes=[pltpu.VMEM((D_MODEL, f), BF16), pltpu.VMEM((D_MODEL, f), BF16)],
        ),
        compiler_params=_cp(("arbitrary", "arbitrary")),
        name="moe_gateup",
    )(tile_e, tile_valid, n_used, xs, w_gu, w_gu, b_gu, b_gu)


def _down_kernel(te_ref, tv_ref, nu_ref, a_ref, wd_ref, bd_ref, o_ref, wd_bf):
    i = pl.program_id(1)
    valid = tv_ref[i]
    new_w = jnp.logical_or(i == 0, te_ref[i] != te_ref[jnp.maximum(i - 1, 0)])

    @pl.when(jnp.logical_and(new_w, valid > 0))
    def _():
        _cast_rows(wd_ref, wd_bf)

    @pl.when(valid == 0)
    def _():
        o_ref[...] = jnp.zeros_like(o_ref)

    @pl.when(valid > 0)
    def _():
        o_ref[...] = _dot(a_ref[...], wd_bf[...]) + bd_ref[...]


def _down(tile_e, tile_valid, n_used, act, w_down, b_down):
    n = MOE_DOWN_COLS

    def tile(i, nu):
        return jnp.minimum(i, nu[0] - 1)

    return pl.pallas_call(
        _down_kernel,
        out_shape=jax.ShapeDtypeStruct((MOE_ROWS, D_MODEL), F32),
        grid_spec=pltpu.PrefetchScalarGridSpec(
            num_scalar_prefetch=3,
            grid=(D_MODEL // n, MOE_MAX_TILES),
            in_specs=[
                pl.BlockSpec((MOE_TILE, EXPERT_FF), lambda j, i, te, tv, nu: (tile(i, nu), 0)),
                pl.BlockSpec((None, EXPERT_FF, n), lambda j, i, te, tv, nu: (te[i], 0, j)),
                pl.BlockSpec((None, 1, n), lambda j, i, te, tv, nu: (te[i], 0, j)),
            ],
            out_specs=pl.BlockSpec((MOE_TILE, n), lambda j, i, te, tv, nu: (i, j)),
            scratch_shapes=[pltpu.VMEM((EXPERT_FF, n), BF16)],
        ),
        compiler_params=_cp(("arbitrary", "arbitrary")),
        name="moe_down",
    )(tile_e, tile_valid, n_used, act, w_down, b_down)


COMBINE_TILE = 128


def _combine_kernel(pos_ref, ys_ref, wt_ref, x1_ref, g2_ref, p2g_ref, o_ref, buf, sem):
    tm = COMBINE_TILE
    i = pl.program_id(0)
    n = pl.num_programs(0)

    def row_copy(slot, k, r, p):
        return pltpu.make_async_copy(ys_ref.at[pl.ds(p, 1)], buf.at[slot, k, pl.ds(r, 1)], sem.at[slot])

    def issue(tile, slot):
        def body(r, carry):
            for k in range(TOP_K):
                row_copy(slot, k, r, pos_ref[k * TOKENS + tile * tm + r]).start()
            return carry

        lax.fori_loop(0, tm, body, 0)

    @pl.when(i == 0)
    def _():
        issue(0, 0)

    @pl.when(i + 1 < n)
    def _():
        issue(i + 1, (i + 1) % 2)

    slot = i % 2

    def drain(r, carry):
        for k in range(TOP_K):
            row_copy(slot, k, r, 0).wait()
        return carry

    lax.fori_loop(0, tm, drain, 0)

    wt = wt_ref[...]
    moe = buf[slot, 0] * wt[:, 0:1]
    for k in range(1, TOP_K):
        moe = moe + buf[slot, k] * wt[:, k:k + 1]
    o_ref[...] = x1_ref[...] + g2_ref[...] * (_rms(moe) * p2g_ref[...])


def _combine(pos_flat, ys, wt_t, x1, mod3, p2g):
    tm = COMBINE_TILE
    per = SEQ // tm
    return pl.pallas_call(
        _combine_kernel,
        out_shape=jax.ShapeDtypeStruct((TOKENS, D_MODEL), F32),
        grid_spec=pltpu.PrefetchScalarGridSpec(
            num_scalar_prefetch=1,
            grid=(TOKENS // tm,),
            in_specs=[
                pl.BlockSpec(memory_space=pl.ANY),
                pl.BlockSpec((tm, TOP_K), lambda i, pos: (i, 0)),
                pl.BlockSpec((tm, D_MODEL), lambda i, pos: (i, 0)),
                pl.BlockSpec((None, 1, D_MODEL), lambda i, pos: ((i // per) * 6 + 5, 0, 0)),
                pl.BlockSpec((1, D_MODEL), lambda i, pos: (0, 0)),
            ],
            out_specs=pl.BlockSpec((tm, D_MODEL), lambda i, pos: (i, 0)),
            scratch_shapes=[pltpu.VMEM((2, TOP_K, tm, D_MODEL), F32), pltpu.SemaphoreType.DMA((2,))],
        ),
        compiler_params=_cp(("arbitrary",)),
        name="moe_combine",
    )(pos_flat, ys, wt_t, x1, mod3, p2g)


def _rope_tables():
    rows = SEQ // GRID_W
    row = jnp.broadcast_to(jnp.arange(rows, dtype=F32)[:, None], (rows, GRID_W)).reshape(-1)
    col = jnp.broadcast_to(jnp.arange(GRID_W, dtype=F32)[None, :], (rows, GRID_W)).reshape(-1)
    n_pairs = ROPE_DIM // 4
    inv_freq = ROPE_BASE ** (-jnp.arange(n_pairs, dtype=F32) / n_pairs)
    ang = jnp.concatenate([row[:, None] * inv_freq, col[:, None] * inv_freq], axis=-1)
    cos, sin = jnp.cos(ang), jnp.sin(ang)
    half = ROPE_DIM // 2
    z = jnp.zeros((SEQ, half), F32)
    ct = jnp.concatenate([cos, cos, z, z], axis=-1)
    sa = jnp.concatenate([-sin, z, z, z], axis=-1)
    sb = jnp.concatenate([z, sin, z, z], axis=-1)
    o = jnp.ones((CTX_LEN, half), F32)
    zc = jnp.zeros((CTX_LEN, half), F32)
    ct_c = jnp.concatenate([o, o, zc, zc], axis=-1)
    z_c = jnp.zeros((CTX_LEN, LANE), F32)
    return (ct, sa, sb), (ct_c, z_c, z_c)


def kernel(x, c, ctx, c_ctx, w_ada, b_ada, pre1_g, post1_g, pre2_g, post2_g, w_in, q_norm_g, w_uq,
           kv_norm_g, w_ukv, w_o_attn, w_dw, b_dw, cln_g, cln_b, w_pw, b_pw, w_out, w_router, b_router,
           w_gu, b_gu, w_down, b_down):
    l = 0
    x2d = x.reshape(TOKENS, D_MODEL)
    ctx2d = ctx.reshape(BATCH * CTX_LEN, D_MODEL)

    w_in_l = w_in[l]
    w_q = w_in_l[:, :KV0].astype(BF16)
    w_kv = jnp.pad(w_in_l[:, KV0:CV0], ((0, 0), (0, KV_COLS - (CV0 - KV0)))).astype(BF16)
    w_b = w_in_l[:, CV0:].astype(BF16)
    w_uq_p = jnp.pad(w_uq[l].reshape(Q_RANK, N_HEADS, NOPE_DIM + ROPE_DIM),
                     ((0, 0), (0, 0), (0, HEAD_PAD - NOPE_DIM - ROPE_DIM)))
    w_uq_p = w_uq_p.reshape(Q_RANK, N_HEADS * HEAD_PAD).astype(BF16)
    w_ukv_r = w_ukv[l].reshape(KV_RANK, N_HEADS, NOPE_DIM + V_DIM)
    w_ukv_p = jnp.concatenate([w_ukv_r[:, :, :NOPE_DIM].reshape(KV_RANK, -1),
                               w_ukv_r[:, :, NOPE_DIM:].reshape(KV_RANK, -1)], axis=-1).astype(BF16)
    w_o = w_o_attn[l].astype(BF16)
    w_pw_b = w_pw[l].astype(BF16)
    w_out_b = w_out[l].astype(BF16)
    w_dw3 = w_dw[l].reshape(CONV_WIDTH, CONV_CB, LANE).transpose(1, 0, 2)
    b_dw3 = b_dw[l].reshape(CONV_CB, 1, LANE)
    wr_t = jnp.pad(w_router[l].T, ((0, ROUTER_ROWS - N_EXPERTS), (0, 0)))
    wr_hi = wr_t.astype(BF16)
    wr_lo = (wr_t - wr_hi.astype(F32)).astype(BF16)
    br = b_router[l].reshape(N_EXPERTS, 1)
    tabs, tabs_ctx = _rope_tables()

    cc = jnp.concatenate([c, c_ctx[None, :], jnp.zeros((3, D_MODEL), F32)], axis=0)
    mod = _ada(cc, w_ada[l], b_ada[l][None, :])
    mod3 = mod.reshape(8 * 6, 1, D_MODEL)

    h = _prenorm(x2d, pre1_g[l][None, :], mod3, lambda i: i // (SEQ // 256), 0, 1)
    hc = _prenorm(ctx2d, pre1_g[l][None, :], mod3, lambda i: 4, 0, 1)
    qg = q_norm_g[l][None, :]
    kvg = kv_norm_g[l][None, :]
    q_cat, k_cat, v = _mla(h, w_q, qg, w_uq_p, w_kv, kvg, w_ukv_p, tabs,
                           lambda i: i % (SEQ // 256), True)
    kc_cat, vc = _mla(hc, None, None, None, w_kv, kvg, w_ukv_p, tabs_ctx, lambda i: 0, False)
    attn = _attention(q_cat, k_cat, v, kc_cat, vc)
    zb = _mm(h, w_b, 512, 1024, name="in_proj_b")
    cv = _conv(zb, w_dw3, b_dw3, cln_g[l][None, :], cln_b[l][None, :])
    merged = _merge(attn, w_o, cv, w_pw_b, b_pw[l][None, :], zb)
    y = _mm(merged, w_out_b, 512, 1024, name="out_proj")

    x1, hp, idx, wts, rank, cnt = _router(y, x2d, post1_g[l][None, :], pre2_g[l][None, :], mod3,
                                          wr_hi, wr_lo, br)
    counts = cnt[:, 0]
    padded = ((counts + MOE_TILE - 1) // MOE_TILE) * MOE_TILE
    ends = jnp.cumsum(padded)
    offs = ends - padded
    pos_flat = (jnp.take(offs, idx) + rank).reshape(-1)
    n_used = (ends[-1] // MOE_TILE).astype(jnp.int32)
    starts = jnp.arange(MOE_MAX_TILES, dtype=jnp.int32) * MOE_TILE
    t_e = jnp.minimum(jnp.searchsorted(ends, starts, side="right"), N_EXPERTS - 1).astype(jnp.int32)
    t_valid = jnp.clip(jnp.take(counts, t_e) - (starts - jnp.take(offs, t_e)), 0, MOE_TILE)
    used = starts < ends[-1]
    last_e = jnp.take(t_e, jnp.maximum(n_used - 1, 0))
    t_e = jnp.where(used, t_e, last_e).astype(jnp.int32)
    t_valid = jnp.where(used, t_valid, 0).astype(jnp.int32)
    n_used1 = n_used.reshape(1)
    meta = jnp.concatenate([offs + counts, padded - counts, n_used1]).astype(jnp.int32)

    xs = _dispatch(pos_flat, meta, hp)
    act = _gateup(t_e, t_valid, n_used1, xs, w_gu[l], b_gu[l][:, None, :])
    ys = _down(t_e, t_valid, n_used1, act, w_down[l], b_down[l][:, None, :])
    out = _combine(pos_flat, ys, wts.T, x1, mod3, post2_g[l][None, :])
    return out.reshape(BATCH, SEQ, D_MODEL)
```
